```python
import jax, jax.numpy as jnp
from jax import lax
import numpy as np

D_MODEL = 1024
BATCH = 4
SEQ = 8192
DEPTH = 2

HEAD_DIM = 64
D_MIX = D_MODEL
A_WIDTH = D_MODEL // 4
A_HEADS = A_WIDTH // HEAD_DIM
CHUNK = 128
B_WIDTH = D_MODEL // 2
B_HEADS = B_WIDTH // HEAD_DIM
IDX_HEADS = 8
IDX_DIM = 64
TOPK_MAX = 256
BLOCK_Q = 128
C_WIDTH = D_MODEL // 4
C_GROUPS = C_WIDTH // HEAD_DIM
CONV_WIDTH = 31
ROPE_THETA = 500000.0
ROPE_DIM = HEAD_DIM // 4
N_EXPERTS = 32
TOP_K_EXPERTS = 4
D_EXPERT = D_MODEL
SWIGLU_LIMIT = 7.0
SWIGLU_ALPHA = 1.702
MOE_BLOCK = 128
EPS = 1e-6
NEG_INF = -1e30
ATTN_SCALE = HEAD_DIM ** -0.5
IDX_W_SCALE = (IDX_HEADS ** -0.5) * (IDX_DIM ** -0.5)
IN_SPLITS = [A_WIDTH, A_WIDTH, B_WIDTH, B_WIDTH, B_WIDTH, IDX_HEADS * IDX_DIM, IDX_DIM, IDX_HEADS, C_WIDTH, C_WIDTH]
IN_COLS = sum(IN_SPLITS)

kernel_name = 'hybrid_parallel_heads_block'


def _rmsnorm(x, g):
    xf = x.astype(jnp.float32)
    y = xf * lax.rsqrt(jnp.mean(xf * xf, axis=-1, keepdims=True) + EPS)
    return (y * g.astype(jnp.float32)).astype(x.dtype)


def _layernorm(x, g, b):
    xf = x.astype(jnp.float32)
    xc = xf - jnp.mean(xf, axis=-1, keepdims=True)
    y = xc * lax.rsqrt(jnp.mean(xc * xc, axis=-1, keepdims=True) + EPS)
    return (y * g.astype(jnp.float32) + b.astype(jnp.float32)).astype(x.dtype)


def _rope_tables(seq):
    pos = jnp.arange(seq, dtype=jnp.float32)
    inv_freq = ROPE_THETA ** (-jnp.arange(0, ROPE_DIM, 2, dtype=jnp.float32) / ROPE_DIM)
    ang = pos[:, None] * inv_freq[None, :]
    return jnp.cos(ang), jnp.sin(ang)


def _partial_rope(x, cos, sin):
    half = ROPE_DIM // 2
    cos = cos.astype(x.dtype)
    sin = sin.astype(x.dtype)
    x1 = x[..., :half]
    x2 = x[..., half:ROPE_DIM]
    return jnp.concatenate([x1 * cos - x2 * sin, x2 * cos + x1 * sin, x[..., ROPE_DIM:]], axis=-1)


def _chunked_sgu(u_pre, v_pre, ln_g, ln_b, w_s, b_s):
    bsz, seq, _ = u_pre.shape
    u = jax.nn.gelu(u_pre)
    v = _layernorm(jax.nn.gelu(v_pre), ln_g, ln_b)
    v = v.reshape(bsz, seq // CHUNK, CHUNK, A_HEADS, HEAD_DIM)
    causal = jnp.tril(jnp.ones((CHUNK, CHUNK), dtype=w_s.dtype))
    mixed = jnp.einsum('hts,bcshd->bcthd', w_s * causal, v) + jnp.transpose(b_s)[:, :, None]
    return u * mixed.reshape(bsz, seq, A_WIDTH)


def _to_blocks(a, n_blocks):
    a = a.reshape((a.shape[0], n_blocks, BLOCK_Q) + a.shape[2:])
    return jnp.moveaxis(a, 1, 0)


def _indexed_sparse_attention(q, k, v, q_idx, k_idx, w_idx, cos, sin):
    bsz, seq = q.shape[:2]
    n_sel = min(TOPK_MAX, seq // 4)
    n_blocks = seq // BLOCK_Q
    cos4, sin4 = cos[:, None, :], sin[:, None, :]
    q = _partial_rope(q.reshape(bsz, seq, B_HEADS, HEAD_DIM), cos4, sin4)
    k = _partial_rope(k.reshape(bsz, seq, B_HEADS, HEAD_DIM), cos4, sin4)
    v = v.reshape(bsz, seq, B_HEADS, HEAD_DIM)
    q_idx = _partial_rope(q_idx.reshape(bsz, seq, IDX_HEADS, IDX_DIM), cos4, sin4)
    k_idx = _partial_rope(k_idx, cos, sin)
    kpos = jnp.arange(seq, dtype=jnp.int32)
    gather = jax.vmap(lambda a, i: a[i])

    def block(args):
        blk, qb, qib, wb = args
        qpos = blk * BLOCK_Q + jnp.arange(BLOCK_Q, dtype=jnp.int32)
        rel = jax.nn.relu(jnp.einsum('bthd,bsd->bths', qib, k_idx)).astype(jnp.float32)
        score = jnp.einsum('bth,bths->bts', wb.astype(jnp.float32) * IDX_W_SCALE, rel)
        admissible = kpos[None, :] <= qpos[:, None]
        score = jnp.where(admissible[None], score, NEG_INF)
        _, sel = lax.top_k(score, n_sel)
        valid = sel <= qpos[None, :, None]
        kg = gather(k, sel)
        vg = gather(v, sel)
        logits = jnp.einsum('bthd,btkhd->bthk', qb, kg).astype(jnp.float32) * ATTN_SCALE
        logits = jnp.where(valid[:, :, None, :], logits, NEG_INF)
        p = jax.nn.softmax(logits, axis=-1).astype(vg.dtype)
        return jnp.einsum('bthk,btkhd->bthd', p, vg)

    out = lax.map(block, (jnp.arange(n_blocks, dtype=jnp.int32), _to_blocks(q, n_blocks),
                          _to_blocks(q_idx, n_blocks), _to_blocks(w_idx, n_blocks)))
    return jnp.moveaxis(out, 0, 1).reshape(bsz, seq, B_WIDTH)


def _conv_module(a_pre, g_pre, conv_w, conv_b, ln_g, ln_b):
    a = a_pre * jax.nn.sigmoid(g_pre)
    a = lax.conv_general_dilated(a, conv_w[:, None, :], window_strides=(1,),
                                 padding=[(CONV_WIDTH - 1, 0)],
                                 dimension_numbers=('NWC', 'WIO', 'NWC'),
                                 feature_group_count=C_WIDTH) + conv_b
    return jax.nn.silu(_layernorm(a, ln_g, ln_b))


def _token_mix(h, w_in, sgu_ln_g, sgu_ln_b, sgu_w, sgu_b, conv_w, conv_b, conv_ln_g, conv_ln_b, w_out, cos, sin):
    proj = h @ w_in
    u_a, v_a, q, k, v, q_idx, k_idx, w_idx, c_a, c_g = jnp.split(proj, np.cumsum(IN_SPLITS)[:-1].tolist(), axis=-1)
    out_a = _chunked_sgu(u_a, v_a, sgu_ln_g, sgu_ln_b, sgu_w, sgu_b)
    out_b = _indexed_sparse_attention(q, k, v, q_idx, k_idx, w_idx, cos, sin)
    out_c = _conv_module(c_a, c_g, conv_w, conv_b, conv_ln_g, conv_ln_b)
    return jnp.concatenate([out_a, out_b, out_c], axis=-1) @ w_out


def _moe(h, router_w, router_b, w1, b1, w2, b2):
    bsz, seq, dm = h.shape
    n_tok = bsz * seq
    xt = h.reshape(n_tok, dm)
    logits = (xt @ router_w + router_b).astype(jnp.float32)
    top_vals, top_idx = lax.top_k(logits, TOP_K_EXPERTS)
    gates = jax.nn.softmax(top_vals, axis=-1)
    nk = n_tok * TOP_K_EXPERTS
    e_flat = top_idx.reshape(nk)
    tok_flat = jnp.arange(nk, dtype=jnp.int32) // TOP_K_EXPERTS
    g_flat = gates.reshape(nk)
    order = jnp.argsort(e_flat)
    e_sorted, tok_sorted, g_sorted = e_flat[order], tok_flat[order], g_flat[order]
    counts = jnp.bincount(e_flat, length=N_EXPERTS)
    padded = ((counts + MOE_BLOCK - 1) // MOE_BLOCK) * MOE_BLOCK
    start = jnp.cumsum(counts) - counts
    pend = jnp.cumsum(padded)
    pstart = pend - padded
    dest = pstart[e_sorted] + (jnp.arange(nk, dtype=jnp.int32) - start[e_sorted])
    n_blocks = -(-nk // MOE_BLOCK) + N_EXPERTS
    cap = n_blocks * MOE_BLOCK
    buf_tok = jnp.zeros((cap,), jnp.int32).at[dest].set(tok_sorted)
    buf_gate = jnp.zeros((cap,), jnp.float32).at[dest].set(g_sorted)
    blk_start = jnp.arange(n_blocks, dtype=pend.dtype) * MOE_BLOCK
    blk_expert = jnp.minimum(jnp.searchsorted(pend, blk_start, side='right'), N_EXPERTS - 1)

    def expert_block(args):
        e, tok, g = args
        hb = xt[tok] @ w1[e] + b1[e]
        x_glu = jnp.minimum(hb[:, 0::2], SWIGLU_LIMIT)
        x_lin = jnp.clip(hb[:, 1::2], -SWIGLU_LIMIT, SWIGLU_LIMIT)
        act = x_glu * jax.nn.sigmoid(SWIGLU_ALPHA * x_glu) * (x_lin + 1)
        yb = act @ w2[e] + b2[e]
        return yb * g[:, None].astype(yb.dtype)

    yb = lax.map(expert_block, (blk_expert, buf_tok.reshape(n_blocks, MOE_BLOCK), buf_gate.reshape(n_blocks, MOE_BLOCK)))
    y = jnp.zeros((n_tok, dm), h.dtype).at[buf_tok].add(yb.reshape(cap, dm))
    return y.reshape(bsz, seq, dm)


def setup_inputs(seed: int = 0) -> dict:
    key = jax.random.key(seed)
    ks = jax.random.split(key, 24)

    def nrm(k, shape, scale):
        return jax.random.normal(k, shape, jnp.float32) * scale

    def gain(k, shape):
        return 1.0 + nrm(k, shape, 0.05)

    L = DEPTH
    return {
        'x': nrm(ks[0], (BATCH, SEQ, D_MODEL), 1.0),
        'c': nrm(ks[1], (BATCH, D_MODEL), 1.0),
        'ada_w': nrm(ks[2], (L, D_MODEL, 6 * D_MODEL), 0.5 * D_MODEL ** -0.5),
        'ada_b': nrm(ks[3], (L, 6 * D_MODEL), 0.02),
        'mix_pre_g': gain(ks[4], (L, D_MODEL)),
        'mix_post_g': gain(ks[5], (L, D_MODEL)),
        'w_in': nrm(ks[6], (L, D_MODEL, IN_COLS), D_MODEL ** -0.5),
        'sgu_ln_g': gain(ks[7], (L, A_WIDTH)),
        'sgu_ln_b': nrm(ks[8], (L, A_WIDTH), 0.02),
        'sgu_w': nrm(ks[9], (L, A_HEADS, CHUNK, CHUNK), 0.5 * CHUNK ** -0.5),
        'sgu_b': 1.0 + nrm(ks[10], (L, A_HEADS, CHUNK), 0.05),
        'conv_w': nrm(ks[11], (L, CONV_WIDTH, C_WIDTH), CONV_WIDTH ** -0.5),
        'conv_b': nrm(ks[12], (L, C_WIDTH), 0.02),
        'conv_ln_g': gain(ks[13], (L, C_WIDTH)),
        'conv_ln_b': nrm(ks[14], (L, C_WIDTH), 0.02),
        'w_out': nrm(ks[15], (L, D_MIX, D_MODEL), D_MIX ** -0.5),
        'ffn_pre_g': gain(ks[16], (L, D_MODEL)),
        'ffn_post_g': gain(ks[17], (L, D_MODEL)),
        'router_w': nrm(ks[18], (L, D_MODEL, N_EXPERTS), D_MODEL ** -0.5),
        'router_b': nrm(ks[19], (L, N_EXPERTS), 0.01),
        'exp_w1': nrm(ks[20], (L, N_EXPERTS, D_MODEL, 2 * D_EXPERT), D_MODEL ** -0.5),
        'exp_b1': nrm(ks[21], (L, N_EXPERTS, 2 * D_EXPERT), 0.02),
        'exp_w2': nrm(ks[22], (L, N_EXPERTS, D_EXPERT, D_MODEL), D_EXPERT ** -0.5),
        'exp_b2': nrm(ks[23], (L, N_EXPERTS, D_MODEL), 0.02),
    }


def reference(x, c, ada_w, ada_b, mix_pre_g, mix_post_g, w_in, sgu_ln_g, sgu_ln_b, sgu_w, sgu_b,
              conv_w, conv_b, conv_ln_g, conv_ln_b, w_out, ffn_pre_g, ffn_post_g,
              router_w, router_b, exp_w1, exp_b1, exp_w2, exp_b2):
    seq = x.shape[1]
    cos, sin = _rope_tables(seq)
    c_act = jax.nn.silu(c)
    for l in range(DEPTH):
        mod = c_act @ ada_w[l] + ada_b[l]
        sh1, sc1, g1, sh2, sc2, g2 = [m[:, None, :] for m in jnp.split(mod, 6, axis=-1)]
        h = _rmsnorm(x, mix_pre_g[l]) * (1 + sc1) + sh1
        m = _token_mix(h, w_in[l], sgu_ln_g[l], sgu_ln_b[l], sgu_w[l], sgu_b[l], conv_w[l], conv_b[l],
                       conv_ln_g[l], conv_ln_b[l], w_out[l], cos, sin)
        x = x + g1 * _rmsnorm(m, mix_post_g[l])
        h = _rmsnorm(x, ffn_pre_g[l]) * (1 + sc2) + sh2
        f = _moe(h, router_w[l], router_b[l], exp_w1[l], exp_b1[l], exp_w2[l], exp_b2[l])
        x = x + g2 * _rmsnorm(f, ffn_post_g[l])
    return x
```

```python
import functools

import numpy as np
import jax
import jax.numpy as jnp
from jax import lax
from jax.experimental import pallas as pl
from jax.experimental.pallas import tpu as pltpu

F32 = jnp.float32
BF16 = jnp.bfloat16
I32 = jnp.int32

HEAD_DIM = 64
CHUNK = 128
A_HEADS = 4
IDX_HEADS = 8
IDX_DIM = 64
TOPK_MAX = 256
CONV_WIDTH = 31
ROPE_THETA = 500000.0
ROPE_DIM = HEAD_DIM // 4
N_EXPERTS = 32
TOP_K_EXPERTS = 4
SWIGLU_LIMIT = 7.0
SWIGLU_ALPHA = 1.702
EPS = 1e-6
NEG_INF = -1e30
ATTN_SCALE = HEAD_DIM ** -0.5
IDX_W_SCALE = (IDX_HEADS ** -0.5) * (IDX_DIM ** -0.5)

LANES = 128
SUBLANES = 8
VMEM_LIMIT = 52 * 1024 * 1024
INT_MIN = -2 ** 31
NEG_BITS = int(np.float32(NEG_INF).view(np.int32))
CONV_HALO = 32
EXPERT_ROWS = 256


def _sigmoid(x):
    return 1.0 / (1.0 + jnp.exp(-x))


def _gelu_tanh(x):
    return x * (0.5 * (1.0 + jnp.tanh(np.float32(np.sqrt(2.0 / np.pi)) * (x + 0.044715 * (x * x * x)))))


def _layernorm(x, g, b):
    xc = x - jnp.mean(x, axis=-1, keepdims=True)
    return xc * lax.rsqrt(jnp.mean(xc * xc, axis=-1, keepdims=True) + EPS) * g + b


def _rms(x):
    return x * lax.rsqrt(jnp.mean(x * x, axis=-1, keepdims=True) + EPS)


def _dot(a, b):
    return jnp.dot(a, b, preferred_element_type=F32)


def _dot_nt(a, b):
    return lax.dot_general(a, b, (((1,), (1,)), ((), ())), preferred_element_type=F32)


def _dot_tn(a, b):
    return lax.dot_general(a, b, (((0,), (0,)), ((), ())), preferred_element_type=F32)


def _adaln_kernel(c_ref, w_ref, b_ref, o_ref):
    c = c_ref[...]
    ca = (c * _sigmoid(c)).astype(BF16)
    o_ref[0] = _dot(ca, w_ref[0].astype(BF16)) + b_ref[0]


def _adaln(c_pad, ada_w, ada_b):
    depth, d, n = ada_w.shape
    tn = 1536 if n % 1536 == 0 else n
    rows = c_pad.shape[0]
    return pl.pallas_call(
        _adaln_kernel,
        grid=(depth, n // tn),
        in_specs=[pl.BlockSpec((rows, d), lambda l, j: (0, 0)),
                  pl.BlockSpec((1, d, tn), lambda l, j: (l, 0, j)),
                  pl.BlockSpec((1, 1, tn), lambda l, j: (l, 0, j))],
        out_specs=pl.BlockSpec((1, rows, tn), lambda l, j: (l, 0, j)),
        out_shape=jax.ShapeDtypeStruct((depth, rows, n), F32),
        compiler_params=pltpu.CompilerParams(vmem_limit_bytes=VMEM_LIMIT),
        name="adaln",
    )(c_pad, ada_w, ada_b.reshape(depth, 1, n))


def _rope(xp, cs, s1, s2):
    cols = []
    for i in range(xp.shape[1] // LANES):
        xc = xp[:, i * LANES:(i + 1) * LANES]
        up = pltpu.roll(xc, LANES - ROPE_DIM // 2, 1)
        dn = pltpu.roll(xc, ROPE_DIM // 2, 1)
        cols.append(xc * cs + up * s1 + dn * s2)
    return cols[0] if len(cols) == 1 else jnp.concatenate(cols, axis=1)


def _inproj_kernel(x_ref, sc_ref, sh_ref, g_ref, w_ref, wwt_ref, cs_ref, s1_ref, s2_ref,
                   lng_ref, lnb_ref, sguw_ref, sgub_ref, cw_ref, cb_ref, clg_ref, clb_ref,
                   oa_ref, q_ref, k_ref, v_ref, qi_ref, ki_ref, wt_ref, oc_ref, cbuf,
                   *, tm, aw, bw, cwid):
    j = pl.program_id(1)
    x = x_ref[0]
    h = (_rms(x) * g_ref[...]) * (1.0 + sc_ref[0]) + sh_ref[0]
    hb = h.astype(BF16)

    def proj(lo, n):
        return _dot(hb, w_ref[:, lo:lo + n])

    o_u, o_v, o_q = 0, aw, 2 * aw
    o_k, o_vv, o_qi = o_q + bw, o_q + 2 * bw, o_q + 3 * bw
    o_ca = o_qi + IDX_HEADS * IDX_DIM
    o_cg, o_ki = o_ca + cwid, o_ca + 2 * cwid

    u = _gelu_tanh(proj(o_u, aw))
    vln = _layernorm(_gelu_tanh(proj(o_v, aw)), lng_ref[...], lnb_ref[...]).astype(BF16)
    row = lax.broadcasted_iota(I32, (CHUNK, CHUNK), 0)
    col = lax.broadcasted_iota(I32, (CHUNK, CHUNK), 1)
    head_of_lane = lax.broadcasted_iota(I32, (CHUNK, aw), 1) // HEAD_DIM
    wtril = [jnp.where(col <= row, sguw_ref[hh], 0.0).astype(BF16) for hh in range(A_HEADS)]
    for c in range(tm // CHUNK):
        vc = vln[c * CHUNK:(c + 1) * CHUNK]
        mixed = jnp.zeros((CHUNK, aw), F32)
        for hh in range(A_HEADS):
            mixed = jnp.where(head_of_lane == hh, _dot(wtril[hh], vc), mixed)
        oa_ref[0, c * CHUNK:(c + 1) * CHUNK, :] = (
            u[c * CHUNK:(c + 1) * CHUNK] * (mixed + sgub_ref[...])).astype(BF16)

    cs, s1, s2 = cs_ref[...], s1_ref[...], s2_ref[...]
    q_ref[0] = (_rope(proj(o_q, bw), cs, s1, s2) * ATTN_SCALE).astype(BF16)
    k_ref[0] = _rope(proj(o_k, bw), cs, s1, s2).astype(BF16)
    v_ref[0] = proj(o_vv, bw).astype(BF16)
    qi_ref[0] = _rope(proj(o_qi, IDX_HEADS * IDX_DIM), cs, s1, s2).astype(BF16)
    kig = proj(o_ki, LANES)
    ki_ref[0] = _rope(kig, cs, s1, s2)[:, :IDX_DIM].astype(BF16)
    wt_ref[0] = _dot_nt(wwt_ref[...], hb) * IDX_W_SCALE

    glu = proj(o_ca, cwid) * _sigmoid(proj(o_cg, cwid))

    @pl.when(j == 0)
    def _():
        cbuf[0:CONV_HALO, :] = jnp.zeros((CONV_HALO, cwid), F32)

    cbuf[CONV_HALO:CONV_HALO + tm, :] = glu
    acc = jnp.zeros((tm, cwid), F32)
    base = CONV_HALO - (CONV_WIDTH - 1)
    for t in range(CONV_WIDTH):
        acc = acc + cbuf[base + t:base + t + tm, :] * cw_ref[t:t + 1, :]
    cbuf[0:CONV_HALO, :] = cbuf[tm:tm + CONV_HALO, :]
    a = _layernorm(acc + cb_ref[...], clg_ref[...], clb_ref[...])
    oc_ref[0] = (a * _sigmoid(a)).astype(BF16)


def _inproj(x, sc, sh, pre_g, w_pack, wwt, rope_tabs, sgu_ln_g, sgu_ln_b, sgu_w, sgu_bias,
            conv_w, conv_b, conv_ln_g, conv_ln_b, *, tm):
    bsz, seq, d = x.shape
    aw = sgu_ln_g.shape[-1]
    cwid = conv_b.shape[-1]
    bw = (w_pack.shape[1] - LANES - 2 * aw - 2 * cwid - IDX_HEADS * IDX_DIM) // 3
    cs, s1, s2 = rope_tabs
    row2 = lambda b, j: (0, 0)
    per_b = lambda b, j: (b, 0, 0)
    tile = lambda b, j: (b, j, 0)
    full = lambda a: pl.BlockSpec(a.shape, row2 if a.ndim == 2 else (lambda b, j: (0, 0, 0)))
    out_shapes = (
        jax.ShapeDtypeStruct((bsz, seq, aw), BF16),
        jax.ShapeDtypeStruct((bsz, seq, bw), BF16),
        jax.ShapeDtypeStruct((bsz, seq, bw), BF16),
        jax.ShapeDtypeStruct((bsz, seq, bw), BF16),
        jax.ShapeDtypeStruct((bsz, seq, IDX_HEADS * IDX_DIM), BF16),
        jax.ShapeDtypeStruct((bsz, seq, IDX_DIM), BF16),
        jax.ShapeDtypeStruct((bsz, IDX_HEADS, seq), F32),
        jax.ShapeDtypeStruct((bsz, seq, cwid), BF16),
    )
    out_specs = (
        pl.BlockSpec((1, tm, aw), tile), pl.BlockSpec((1, tm, bw), tile), pl.BlockSpec((1, tm, bw), tile),
        pl.BlockSpec((1, tm, bw), tile), pl.BlockSpec((1, tm, IDX_HEADS * IDX_DIM), tile),
        pl.BlockSpec((1, tm, IDX_DIM), tile), pl.BlockSpec((1, IDX_HEADS, tm), lambda b, j: (b, 0, j)),
        pl.BlockSpec((1, tm, cwid), tile),
    )
    rope_spec = pl.BlockSpec((tm, LANES), lambda b, j: (j, 0))
    return pl.pallas_call(
        functools.partial(_inproj_kernel, tm=tm, aw=aw, bw=bw, cwid=cwid),
        grid=(bsz, seq // tm),
        in_specs=[pl.BlockSpec((1, tm, d), tile), pl.BlockSpec((1, 1, d), per_b), pl.BlockSpec((1, 1, d), per_b),
                  full(pre_g), full(w_pack), full(wwt), rope_spec, rope_spec, rope_spec,
                  full(sgu_ln_g), full(sgu_ln_b), full(sgu_w), full(sgu_bias),
                  full(conv_w), full(conv_b), full(conv_ln_g), full(conv_ln_b)],
        out_specs=out_specs,
        out_shape=out_shapes,
        scratch_shapes=[pltpu.VMEM((tm + CONV_HALO, cwid), F32)],
        compiler_params=pltpu.CompilerParams(vmem_limit_bytes=VMEM_LIMIT,
                                             dimension_semantics=("arbitrary", "arbitrary")),
        name="inproj",
    )(x, sc, sh, pre_g, w_pack, wwt, cs, s1, s2, sgu_ln_g, sgu_ln_b, sgu_w, sgu_bias,
      conv_w, conv_b, conv_ln_g, conv_ln_b)


def _attn_kernel(q_ref, qi_ref, wt_ref, k_ref, v_ref, ki_ref, o_ref, keys, acc, m_s, l_s,
                 *, tq, n_sel, n_heads):
    qb = pl.program_id(1)
    n_tiles = qb + 1
    n_pairs = n_heads // 2
    lane_half = lax.broadcasted_iota(I32, (tq, LANES), 1) // HEAD_DIM

    def head_views(ref):
        out = []
        for hh in range(n_heads):
            pair = ref[0, :, (hh // 2) * LANES:(hh // 2 + 1) * LANES]
            out.append(jnp.where(lane_half == hh % 2, pair, jnp.zeros_like(pair)))
        return out

    qpos = qb * tq + lax.broadcasted_iota(I32, (1, tq), 1)

    qi_h = head_views(qi_ref)
    w_all = wt_ref[0]

    def score_tile(t, carry):
        r0 = pl.multiple_of(t * tq, tq)
        ki = ki_ref[0, pl.ds(r0, tq), :]
        ki2 = jnp.concatenate([ki, ki], axis=1)
        s = jnp.zeros((tq, tq), F32)
        for hh in range(IDX_HEADS):
            s = s + w_all[hh:hh + 1, :] * jnp.maximum(_dot_nt(ki2, qi_h[hh]), 0.0)
        bits = pltpu.bitcast(s, I32)
        key = bits ^ ((bits >> 31) & 0x7FFFFFFF)
        kpos = r0 + lax.broadcasted_iota(I32, (tq, tq), 0)
        keys[pl.ds(r0, tq), :] = jnp.where(kpos <= qpos, key, INT_MIN)
        return carry

    lax.fori_loop(0, n_tiles, score_tile, 0)

    need = jnp.minimum(qpos + 1, n_sel)

    def count_ge(cand):
        def body(t, cnt):
            r0 = pl.multiple_of(t * tq, tq)
            hit = jnp.where(keys[pl.ds(r0, tq), :] >= cand, 1, 0)
            return cnt + jnp.sum(hit.reshape(tq // SUBLANES, SUBLANES, tq), axis=0)
        cnt = lax.fori_loop(0, n_tiles, body, jnp.zeros((SUBLANES, tq), I32))
        return jnp.sum(cnt, axis=0, keepdims=True)

    def bisect(i, thr):
        cand = thr + lax.shift_left(jnp.int32(1), 31 - i)
        return jnp.where(count_ge(cand) >= need, cand, thr)

    thr = lax.fori_loop(0, 32, bisect, jnp.full((1, tq), INT_MIN, I32))

    def to_bias(t, carry):
        r0 = pl.multiple_of(t * tq, tq)
        keys[pl.ds(r0, tq), :] = jnp.where(keys[pl.ds(r0, tq), :] >= thr, 0, NEG_BITS)
        return carry

    lax.fori_loop(0, n_tiles, to_bias, 0)

    q_h = head_views(q_ref)
    m_s[...] = jnp.full(m_s.shape, NEG_INF, F32)
    l_s[...] = jnp.zeros(l_s.shape, F32)
    acc[...] = jnp.zeros(acc.shape, F32)

    def attend(t, carry):
        r0 = pl.multiple_of(t * tq, tq)
        bias = pltpu.bitcast(keys[pl.ds(r0, tq), :], F32)
        for hh in range(n_heads):
            pr = hh // 2
            kp = k_ref[0, pl.ds(r0, tq), pr * LANES:(pr + 1) * LANES]
            vp = v_ref[0, pl.ds(r0, tq), pr * LANES:(pr + 1) * LANES]
            lt = _dot_nt(kp, q_h[hh]) + bias
            m_old = m_s[hh:hh + 1, :]
            m_new = jnp.maximum(m_old, jnp.max(lt, axis=0, keepdims=True))
            alpha = jnp.exp(m_old - m_new)
            p = jnp.exp(lt - m_new)
            l_s[hh:hh + 1, :] = alpha * l_s[hh:hh + 1, :] + jnp.sum(p, axis=0, keepdims=True)
            pv = _dot_tn(vp, p.astype(BF16))
            lo = (hh % 2) * HEAD_DIM
            acc[hh * HEAD_DIM:(hh + 1) * HEAD_DIM, :] = (
                alpha * acc[hh * HEAD_DIM:(hh + 1) * HEAD_DIM, :] + pv[lo:lo + HEAD_DIM, :])
            m_s[hh:hh + 1, :] = m_new
        return carry

    lax.fori_loop(0, n_tiles, attend, 0)

    outs = []
    for hh in range(n_heads):
        outs.append(acc[hh * HEAD_DIM:(hh + 1) * HEAD_DIM, :] / l_s[hh:hh + 1, :])
    o_ref[0] = jnp.concatenate(outs, axis=0).T.astype(BF16)


def _attention(q, k, v, qi, ki, wt, *, tq):
    bsz, seq, bw = q.shape
    n_heads = bw // HEAD_DIM
    n_sel = min(TOPK_MAX, seq // 4)
    qtile = lambda b, i: (b, i, 0)
    per_b = lambda b, i: (b, 0, 0)
    whole = lambda shape: pl.BlockSpec(shape, per_b, pipeline_mode=pl.Buffered(1))
    return pl.pallas_call(
        functools.partial(_attn_kernel, tq=tq, n_sel=n_sel, n_heads=n_heads),
        grid=(bsz, seq // tq),
        in_specs=[pl.BlockSpec((1, tq, bw), qtile), pl.BlockSpec((1, tq, IDX_HEADS * IDX_DIM), qtile),
                  pl.BlockSpec((1, IDX_HEADS, tq), lambda b, i: (b, 0, i)),
                  whole((1, seq, bw)), whole((1, seq, bw)), whole((1, seq, IDX_DIM))],
        out_specs=pl.BlockSpec((1, tq, bw), qtile),
        out_shape=jax.ShapeDtypeStruct((bsz, seq, bw), BF16),
        scratch_shapes=[pltpu.VMEM((seq, tq), I32), pltpu.VMEM((bw, tq), F32),
                        pltpu.VMEM((n_heads, tq), F32), pltpu.VMEM((n_heads, tq), F32)],
        compiler_params=pltpu.CompilerParams(vmem_limit_bytes=VMEM_LIMIT,
                                             dimension_semantics=("arbitrary", "arbitrary")),
        name="sparse_attn",
    )(q, qi, wt, k, v, ki)


def _outproj_kernel(oa_ref, ob_ref, oc_ref, x_ref, g1_ref, sc_ref, sh_ref, pg_ref, fg_ref, w_ref,
                    rw_ref, rb_ref, x1_ref, h2_ref, idx_ref, gate_ref, rank_ref, cnt_ref, carry,
                    *, tm, aw, bw):
    first = jnp.logical_and(pl.program_id(0) == 0, pl.program_id(1) == 0)

    @pl.when(first)
    def _():
        carry[...] = jnp.zeros(carry.shape, F32)

    m = (_dot(oa_ref[0], w_ref[0:aw, :]) + _dot(ob_ref[0], w_ref[aw:aw + bw, :])
         + _dot(oc_ref[0], w_ref[aw + bw:, :]))
    x1 = x_ref[0] + g1_ref[0] * (_rms(m) * pg_ref[...])
    x1_ref[0] = x1
    h2 = (_rms(x1) * fg_ref[...]) * (1.0 + sc_ref[0]) + sh_ref[0]
    h2_ref[0] = h2

    lane = lax.broadcasted_iota(I32, (tm, LANES), 1)
    lg = jnp.where(lane < N_EXPERTS, _dot(h2.astype(BF16), rw_ref[...]) + rb_ref[...], -jnp.inf)
    vals, hots = [], []
    idx_out = jnp.zeros((tm, LANES), I32)
    for kk in range(TOP_K_EXPERTS):
        mx = jnp.max(lg, axis=1, keepdims=True)
        ix = jnp.min(jnp.where(lg == mx, lane, LANES), axis=1, keepdims=True)
        hot = lane == ix
        vals.append(mx)
        hots.append(hot)
        idx_out = jnp.where(lane == kk, ix, idx_out)
        lg = jnp.where(hot, -jnp.inf, lg)
    es = [jnp.exp(vv - vals[0]) for vv in vals]
    den = es[0] + es[1] + es[2] + es[3]
    gate_out = jnp.zeros((tm, LANES), F32)
    sel = jnp.zeros((tm, LANES), F32)
    for kk in range(TOP_K_EXPERTS):
        gate_out = jnp.where(lane == kk, es[kk] / den, gate_out)
        sel = jnp.where(hots[kk], 1.0, sel)
    row = lax.broadcasted_iota(I32, (tm, tm), 0)
    col = lax.broadcasted_iota(I32, (tm, tm), 1)
    lower = jnp.where(col < row, 1.0, 0.0).astype(BF16)
    prefix = _dot(lower, sel.astype(BF16)) + carry[...]
    rank_out = jnp.zeros((tm, LANES), F32)
    for kk in range(TOP_K_EXPERTS):
        rk = jnp.sum(jnp.where(hots[kk], prefix, 0.0), axis=1, keepdims=True)
        rank_out = jnp.where(lane == kk, rk, rank_out)
    carry[...] = carry[...] + jnp.sum(sel, axis=0, keepdims=True)
    idx_ref[0] = idx_out[:, :TOP_K_EXPERTS]
    gate_ref[0] = gate_out[:, :TOP_K_EXPERTS]
    rank_ref[0] = rank_out[:, :TOP_K_EXPERTS].astype(I32)
    cnt_ref[...] = carry[...].astype(I32)


def _outproj(out_a, out_b, out_c, x, g1, sc2, sh2, post_g, ffn_pre_g, w_out, rw, rb, *, tm):
    bsz, seq, d = x.shape
    aw, bw = out_a.shape[-1], out_b.shape[-1]
    tile = lambda b, j: (b, j, 0)
    per_b = lambda b, j: (b, 0, 0)
    full = lambda a: pl.BlockSpec(a.shape, lambda b, j: (0, 0))
    k4 = TOP_K_EXPERTS
    return pl.pallas_call(
        functools.partial(_outproj_kernel, tm=tm, aw=aw, bw=bw),
        grid=(bsz, seq // tm),
        in_specs=[pl.BlockSpec((1, tm, aw), tile), pl.BlockSpec((1, tm, bw), tile),
                  pl.BlockSpec((1, tm, out_c.shape[-1]), tile), pl.BlockSpec((1, tm, d), tile),
                  pl.BlockSpec((1, 1, d), per_b), pl.BlockSpec((1, 1, d), per_b), pl.BlockSpec((1, 1, d), per_b),
                  full(post_g), full(ffn_pre_g), full(w_out), full(rw), full(rb)],
        out_specs=(pl.BlockSpec((1, tm, d), tile), pl.BlockSpec((1, tm, d), tile),
                   pl.BlockSpec((1, tm, k4), tile), pl.BlockSpec((1, tm, k4), tile),
                   pl.BlockSpec((1, tm, k4), tile), pl.BlockSpec((1, LANES), lambda b, j: (0, 0))),
        out_shape=(jax.ShapeDtypeStruct((bsz, seq, d), F32), jax.ShapeDtypeStruct((bsz, seq, d), F32),
                   jax.ShapeDtypeStruct((bsz, seq, k4), I32), jax.ShapeDtypeStruct((bsz, seq, k4), F32),
                   jax.ShapeDtypeStruct((bsz, seq, k4), I32), jax.ShapeDtypeStruct((1, LANES), I32)),
        scratch_shapes=[pltpu.VMEM((1, LANES), F32)],
        compiler_params=pltpu.CompilerParams(vmem_limit_bytes=VMEM_LIMIT,
                                             dimension_semantics=("arbitrary", "arbitrary")),
        name="outproj_router",
    )(out_a, out_b, out_c, x, g1, sc2, sh2, post_g, ffn_pre_g, w_out, rw, rb)


def _dispatch_kernel(dest_ref, h_ref, xs_in_ref, xs_ref, sem, *, tt):
    del xs_in_ref
    t0 = pl.program_id(0) * tt

    def row_copy(t, kk):
        flat = t * TOP_K_EXPERTS + kk
        d = dest_ref[flat // LANES, flat % LANES]
        return pltpu.make_async_copy(h_ref.at[t0 + t], xs_ref.at[d], sem)

    def start(t, c):
        for kk in range(TOP_K_EXPERTS):
            row_copy(t, kk).start()
        return c

    def wait(t, c):
        for kk in range(TOP_K_EXPERTS):
            row_copy(t, kk).wait()
        return c

    lax.fori_loop(0, tt, start, 0)
    lax.fori_loop(0, tt, wait, 0)


def _dispatch(dest2d, h2, xs_zero, *, tt):
    n_tok, d = h2.shape
    rows = tt * TOP_K_EXPERTS // LANES
    return pl.pallas_call(
        functools.partial(_dispatch_kernel, tt=tt),
        grid=(n_tok // tt,),
        in_specs=[pl.BlockSpec((rows, LANES), lambda i: (i, 0), memory_space=pltpu.SMEM),
                  pl.BlockSpec(memory_space=pl.ANY), pl.BlockSpec(memory_space=pl.ANY)],
        out_specs=pl.BlockSpec(memory_space=pl.ANY),
        out_shape=jax.ShapeDtypeStruct(xs_zero.shape, xs_zero.dtype),
        scratch_shapes=[pltpu.SemaphoreType.DMA(())],
        input_output_aliases={2: 0},
        compiler_params=pltpu.CompilerParams(dimension_semantics=("arbitrary",), has_side_effects=True),
        name="moe_dispatch",
    )(dest2d, h2, xs_zero)


def _expert_kernel(be_ref, nu_ref, xs_ref, w1g_ref, w1l_ref, b1g_ref, b1l_ref, w2_ref, b2_ref, y_ref):
    del be_ref

    @pl.when(pl.program_id(0) < nu_ref[0])
    def _():
        xb = xs_ref[...].astype(BF16)
        x_glu = jnp.minimum(_dot(xb, w1g_ref[0]) + b1g_ref[0], SWIGLU_LIMIT)
        x_lin = jnp.clip(_dot(xb, w1l_ref[0]) + b1l_ref[0], -SWIGLU_LIMIT, SWIGLU_LIMIT)
        act = x_glu * _sigmoid(SWIGLU_ALPHA * x_glu) * (x_lin + 1.0)
        y_ref[...] = _dot(act.astype(BF16), w2_ref[0]) + b2_ref[0]


def _experts(blk_expert, n_used, xs, w1g, w1l, b1g, b1l, w2, b2):
    cap, d = xs.shape
    de = w1g.shape[-1]
    n_blocks = cap // EXPERT_ROWS
    rows = lambda i, be, nu: (jnp.minimum(i, nu[0] - 1), 0)
    wsel = lambda i, be, nu: (be[i], 0, 0)
    grid_spec = pltpu.PrefetchScalarGridSpec(
        num_scalar_prefetch=2,
        grid=(n_blocks,),
        in_specs=[pl.BlockSpec((EXPERT_ROWS, d), rows),
                  pl.BlockSpec((1, d, de), wsel), pl.BlockSpec((1, d, de), wsel),
                  pl.BlockSpec((1, 1, de), wsel), pl.BlockSpec((1, 1, de), wsel),
                  pl.BlockSpec((1, de, d), wsel), pl.BlockSpec((1, 1, d), wsel)],
        out_specs=pl.BlockSpec((EXPERT_ROWS, d), rows),
    )
    return pl.pallas_call(
        _expert_kernel,
        grid_spec=grid_spec,
        out_shape=jax.ShapeDtypeStruct((cap, d), F32),
        compiler_params=pltpu.CompilerParams(vmem_limit_bytes=VMEM_LIMIT, dimension_semantics=("arbitrary",)),
        name="moe_experts",
    )(blk_expert, n_used, xs, w1g, w1l, b1g, b1l, w2, b2)


def _combine_kernel(dest_ref, gate_ref, x1_ref, g2_ref, pg_ref, y_ref, o_ref, buf, sem, *, tt):
    def row_copy(t, kk):
        flat = t * TOP_K_EXPERTS + kk
        d = dest_ref[flat // LANES, flat % LANES]
        return pltpu.make_async_copy(y_ref.at[d], buf.at[kk, t], sem)

    def start(t, c):
        for kk in range(TOP_K_EXPERTS):
            row_copy(t, kk).start()
        return c

    def wait(t, c):
        for kk in range(TOP_K_EXPERTS):
            row_copy(t, kk).wait()
        return c

    lax.fori_loop(0, tt, start, 0)
    lax.fori_loop(0, tt, wait, 0)
    g = gate_ref[...]
    f = buf[0] * g[:, 0:1]
    for kk in range(1, TOP_K_EXPERTS):
        f = f + buf[kk] * g[:, kk:kk + 1]
    o_ref[...] = x1_ref[...] + g2_ref[0] * (_rms(f) * pg_ref[...])


def _combine(dest2d, gates, x1, g2, post_g, yb, *, tt, seq):
    n_tok, d = x1.shape
    rows = tt * TOP_K_EXPERTS // LANES
    per_seq = seq // tt
    return pl.pallas_call(
        functools.partial(_combine_kernel, tt=tt),
        grid=(n_tok // tt,),
        in_specs=[pl.BlockSpec((rows, LANES), lambda i: (i, 0), memory_space=pltpu.SMEM),
                  pl.BlockSpec((tt, TOP_K_EXPERTS), lambda i: (i, 0)),
                  pl.BlockSpec((tt, d), lambda i: (i, 0)),
                  pl.BlockSpec((1, 1, d), lambda i: (i // per_seq, 0, 0)),
                  pl.BlockSpec((1, d), lambda i: (0, 0)),
                  pl.BlockSpec(memory_space=pl.ANY)],
        out_specs=pl.BlockSpec((tt, d), lambda i: (i, 0)),
        out_shape=jax.ShapeDtypeStruct((n_tok, d), F32),
        scratch_shapes=[pltpu.VMEM((TOP_K_EXPERTS, tt, d), F32), pltpu.SemaphoreType.DMA(())],
        compiler_params=pltpu.CompilerParams(vmem_limit_bytes=VMEM_LIMIT, dimension_semantics=("arbitrary",)),
        name="moe_combine",
    )(dest2d, gates, x1, g2, post_g, yb)


def _rope_lane_tables(seq):
    pos = jnp.arange(seq, dtype=F32)
    inv_freq = ROPE_THETA ** (-jnp.arange(0, ROPE_DIM, 2, dtype=F32) / ROPE_DIM)
    ang = pos[:, None] * inv_freq[None, :]
    cos, sin = jnp.cos(ang), jnp.sin(ang)
    half = ROPE_DIM // 2
    ones = jnp.ones((seq, HEAD_DIM - ROPE_DIM), F32)
    zeros = jnp.zeros((seq, HEAD_DIM - ROPE_DIM), F32)
    zh = jnp.zeros((seq, half), F32)
    cs = jnp.concatenate([cos, cos, ones], axis=1)
    s1 = jnp.concatenate([-sin, zh, zeros], axis=1)
    s2 = jnp.concatenate([zh, sin, zeros], axis=1)
    rep = LANES // HEAD_DIM
    return tuple(jnp.tile(t, (1, rep)) for t in (cs, s1, s2))


def _largest_tile(n, cap):
    t = cap
    while n % t:
        t //= 2
    return t


def kernel(x, c, ada_w, ada_b, mix_pre_g, mix_post_g, w_in, sgu_ln_g, sgu_ln_b, sgu_w, sgu_b, conv_w, conv_b, conv_ln_g, conv_ln_b, w_out, ffn_pre_g, ffn_post_g, router_w, router_b, exp_w1, exp_b1, exp_w2, exp_b2):
    bsz, seq, d = x.shape
    depth = ada_w.shape[0]
    aw = sgu_ln_g.shape[-1]
    cwid = conv_b.shape[-1]
    bw = d - aw - cwid
    n_tok = bsz * seq
    tm = _largest_tile(seq, 512)
    tq = _largest_tile(seq, 256)
    tt = _largest_tile(n_tok, 256)

    c_pad = jnp.zeros((SUBLANES, d), F32).at[:bsz].set(c)
    mod = _adaln(c_pad, ada_w, ada_b)
    rope_tabs = _rope_lane_tables(seq)

    splits = [aw, aw, bw, bw, bw, IDX_HEADS * IDX_DIM, IDX_DIM, IDX_HEADS, cwid, cwid]
    offs = np.concatenate([[0], np.cumsum(splits)]).tolist()

    nk = n_tok * TOP_K_EXPERTS
    n_blocks = -(-nk // EXPERT_ROWS) + N_EXPERTS
    cap = n_blocks * EXPERT_ROWS

    for l in range(depth):
        m6 = mod[l, :bsz].reshape(bsz, 1, 6, d)
        sh1, sc1, g1, sh2, sc2, g2 = [m6[:, :, i, :] for i in range(6)]
        wl = w_in[l]
        seg = lambda i: wl[:, offs[i]:offs[i + 1]]
        pad = jnp.zeros((d, LANES - IDX_DIM - IDX_HEADS), F32)
        w_pack = jnp.concatenate([seg(0), seg(1), seg(2), seg(3), seg(4), seg(5), seg(8), seg(9),
                                  seg(6), seg(7), pad], axis=1).astype(BF16)
        wwt = seg(7).T.astype(BF16)
        sgu_bias = jnp.repeat(sgu_b[l].T, HEAD_DIM, axis=1)
        conv_w_pad = jnp.zeros((CONV_HALO, cwid), F32).at[:CONV_WIDTH].set(conv_w[l])
        r2 = lambda a: a.reshape(1, -1)

        out_a, q, k, v, qi, ki, wt, out_c = _inproj(
            x, sc1, sh1, r2(mix_pre_g[l]), w_pack, wwt, rope_tabs, r2(sgu_ln_g[l]), r2(sgu_ln_b[l]),
            sgu_w[l], sgu_bias, conv_w_pad, r2(conv_b[l]), r2(conv_ln_g[l]), r2(conv_ln_b[l]), tm=tm)
        out_b = _attention(q, k, v, qi, ki, wt, tq=tq)

        rw = jnp.zeros((d, LANES), F32).at[:, :N_EXPERTS].set(router_w[l]).astype(BF16)
        rb = jnp.zeros((1, LANES), F32).at[0, :N_EXPERTS].set(router_b[l])
        x1, h2, top_idx, gates, rank, counts = _outproj(
            out_a, out_b, out_c, x, g1, sc2, sh2, r2(mix_post_g[l]), r2(ffn_pre_g[l]),
            w_out[l].astype(BF16), rw, rb, tm=tm)

        counts = counts[0, :N_EXPERTS]
        padded = ((counts + EXPERT_ROWS - 1) // EXPERT_ROWS) * EXPERT_ROWS
        pend = jnp.cumsum(padded)
        pstart = pend - padded
        dest = pstart[top_idx.reshape(n_tok, TOP_K_EXPERTS)] + rank.reshape(n_tok, TOP_K_EXPERTS)
        dest2d = dest.reshape(nk // LANES, LANES).astype(I32)
        blk_start = jnp.arange(n_blocks, dtype=pend.dtype) * EXPERT_ROWS
        blk_expert = jnp.minimum(jnp.searchsorted(pend, blk_start, side='right'), N_EXPERTS - 1).astype(I32)
        n_used = (pend[-1:] // EXPERT_ROWS).astype(I32)

        xs = _dispatch(dest2d, h2.reshape(n_tok, d), jnp.zeros((cap, d), F32), tt=tt)
        w1 = exp_w1[l]
        b1 = exp_b1[l]
        yb = _experts(blk_expert, n_used, xs,
                      w1[:, :, 0::2].astype(BF16), w1[:, :, 1::2].astype(BF16),
                      b1[:, None, 0::2], b1[:, None, 1::2],
                      exp_w2[l].astype(BF16), exp_b2[l][:, None, :])
        x = _combine(dest2d, gates.reshape(n_tok, TOP_K_EXPERTS), x1.reshape(n_tok, d), g2,
                     r2(ffn_post_g[l]), yb, tt=tt, seq=seq).reshape(bsz, seq, d)
    return x
```
